```python
import functools
import jax, jax.numpy as jnp
from jax import lax
import numpy as np

D_MODEL = 2048
BATCH = 1
SEQ = 8192
DEPTH = 2
DEC_BATCH = 128
DEC_SEQ = 8
PAST_LEN = 16384
PAGE_SIZE = 128

N_META = 16
EPS = 1e-6
MLA_HEADS = 8
QK_NOPE = 128
QK_ROPE = 64
QK_DIM = QK_NOPE + QK_ROPE
V_DIM = 128
MLA_W = MLA_HEADS * V_DIM
Q_LORA = 384
KV_LORA = 128
ROPE_THETA = 10000.0
SM_SCALE = QK_DIM ** -0.5
Q_BLOCK = 128
POOL_WINDOWS = (2, 4, 8, 16)
POOL_GROUPS = 4
POOL_W = 1024
POOL_GROUP = POOL_W // POOL_GROUPS
POOL_HIST = 15
HG_W = 1024
HG_HEAD = 128
HG_HEADS = HG_W // HG_HEAD
HG_CHUNK = 64
D_FF = 5632
CONV_W = 3
N_BRANCH = 3
IN_SIZES = (Q_LORA, KV_LORA, QK_ROPE, POOL_W, HG_W, HG_W, HG_W, HG_W, N_BRANCH * D_MODEL)
N_IN = sum(IN_SIZES)

kernel_name = "hybrid_mla_pool_hgrn2_decode_step"


def rmsnorm(x, g):
    xf = x.astype(jnp.float32)
    y = xf * lax.rsqrt(jnp.mean(xf * xf, axis=-1, keepdims=True) + EPS)
    return (y * g.astype(jnp.float32)).astype(x.dtype)


def split_in(z):
    idx = np.cumsum(IN_SIZES)[:-1].tolist()
    return jnp.split(z, idx, axis=-1)


def rope_cos_sin(pos):
    inv = ROPE_THETA ** (-jnp.arange(0, QK_ROPE, 2, dtype=jnp.float32) / QK_ROPE)
    ang = pos.astype(jnp.float32)[:, None] * inv[None, :]
    return jnp.cos(ang), jnp.sin(ang)


def apply_rope(x, cos, sin):
    xf = x.astype(jnp.float32)
    x1, x2 = xf[..., :QK_ROPE // 2], xf[..., QK_ROPE // 2:]
    return jnp.concatenate([x1 * cos - x2 * sin, x2 * cos + x1 * sin], axis=-1).astype(x.dtype)


def mla_prompt(q_nope, q_rope, c, k_rope, w_uk, w_uv):
    B, T = c.shape[:2]
    k_nope = jnp.einsum("btc,chd->bthd", c, w_uk)
    v = jnp.einsum("btc,chd->bthd", c, w_uv)
    n_blk = -(-T // Q_BLOCK)
    pad = n_blk * Q_BLOCK - T

    def blocks(a):
        a = jnp.pad(a, ((0, 0), (0, pad), (0, 0), (0, 0)))
        return a.reshape(B, n_blk, Q_BLOCK, MLA_HEADS, a.shape[-1]).swapaxes(0, 1)

    key_pos = jnp.arange(T)

    def one_block(args):
        i, qn, qr = args
        s = jnp.einsum("bqhd,bkhd->bhqk", qn, k_nope) + jnp.einsum("bqhr,bkr->bhqk", qr, k_rope)
        s = s.astype(jnp.float32) * SM_SCALE
        q_pos = i * Q_BLOCK + jnp.arange(Q_BLOCK)
        s = jnp.where((key_pos[None, :] <= q_pos[:, None])[None, None], s, -jnp.inf)
        p = jax.nn.softmax(s, axis=-1).astype(v.dtype)
        return jnp.einsum("bhqk,bkhd->bqhd", p, v)

    o = lax.map(one_block, (jnp.arange(n_blk), blocks(q_nope), blocks(q_rope)))
    return o.swapaxes(0, 1).reshape(B, n_blk * Q_BLOCK, MLA_W)[:, :T]


def mla_sample(q_nope, q_rope, c_new, kr_new, w_uk, w_uv, *, cache_c, cache_kr, page_table, layer):
    B, T = c_new.shape[:2]
    q_lat = jnp.einsum("bthd,chd->bthc", q_nope, w_uk)
    causal = jnp.arange(T)[:, None] >= jnp.arange(T)[None, :]

    def one_seq(args):
        pages, ql, qr, cn, krn = args
        cp = cache_c[layer, pages].reshape(-1, KV_LORA)
        krp = cache_kr[layer, pages].reshape(-1, QK_ROPE)
        s_past = jnp.einsum("thc,kc->htk", ql, cp) + jnp.einsum("thr,kr->htk", qr, krp)
        s_new = jnp.einsum("thc,kc->htk", ql, cn) + jnp.einsum("thr,kr->htk", qr, krn)
        s_new = jnp.where(causal[None], s_new, -jnp.inf)
        s = jnp.concatenate([s_past, s_new], axis=-1).astype(jnp.float32) * SM_SCALE
        p = jax.nn.softmax(s, axis=-1).astype(cp.dtype)
        n_past = cp.shape[0]
        return (jnp.einsum("htk,kc->thc", p[..., :n_past], cp)
                + jnp.einsum("htk,kc->thc", p[..., n_past:], cn))

    lat = lax.map(one_seq, (page_table, q_lat, q_rope, c_new, kr_new))
    return jnp.einsum("bthc,chd->bthd", lat, w_uv).reshape(B, T, MLA_W)


def pool_mix(u_hist, u_new, pos, pool_w, pool_scale):
    B, T, _ = u_new.shape
    full = jnp.concatenate([u_hist.astype(u_new.dtype), u_new], axis=1)
    cs = jnp.pad(jnp.cumsum(full.astype(jnp.float32), axis=1), ((0, 0), (1, 0), (0, 0)))
    outs = []
    for g, w in enumerate(POOL_WINDOWS):
        sl = slice(g * POOL_GROUP, (g + 1) * POOL_GROUP)
        win_sum = (cs[:, POOL_HIST + 1:POOL_HIST + 1 + T, sl]
                   - cs[:, POOL_HIST + 1 - w:POOL_HIST + 1 - w + T, sl])
        cnt = jnp.minimum(w, pos + 1).astype(jnp.float32)[None, :, None]
        outs.append(win_sum / cnt)
    d = (jnp.concatenate(outs, axis=-1) - u_new.astype(jnp.float32)).astype(u_new.dtype)
    d = d.reshape(B, T, POOL_GROUPS, POOL_GROUP)
    y = jnp.einsum("btgc,gcd->btgd", d, pool_w).reshape(B, T, POOL_W) * pool_scale
    return y, full[:, -POOL_HIST:]


def gla_chunk(S, q, k, v, logf):
    S = S.astype(jnp.float32)
    L = q.shape[1]
    b = jnp.cumsum(logf, axis=1)
    mask = jnp.arange(L)[:, None] >= jnp.arange(L)[None, :]
    diff = b[:, :, None] - b[:, None, :]
    decay = jnp.exp(jnp.where(mask[None, :, :, None, None], diff, -jnp.inf))
    A = jnp.einsum("bihk,bjhk,bijhk->bhij", q, k, decay)
    o = jnp.einsum("bhij,bjhv->bihv", A, v) + jnp.einsum("bihk,bhkv->bihv", q * jnp.exp(b), S)
    b_last = b[:, -1]
    S_new = (jnp.exp(b_last)[..., None] * S
             + jnp.einsum("bjhk,bjhv->bhkv", k * jnp.exp(b_last[:, None] - b), v))
    return o, S_new


def hgrn_prompt_run(q, k, v, logf):
    B, T, H, K = q.shape
    S0 = jnp.zeros((B, H, K, HG_HEAD), jnp.float32)
    o_meta, S = gla_chunk(S0, q[:, :N_META], k[:, :N_META], v[:, :N_META], logf[:, :N_META])
    n = (T - N_META) // HG_CHUNK

    def to_chunks(a):
        return a[:, N_META:].reshape(B, n, HG_CHUNK, H, a.shape[-1]).swapaxes(0, 1)

    def step(S, xs):
        o, S = gla_chunk(S, *xs)
        return S, o

    S, o_rest = lax.scan(step, S, (to_chunks(q), to_chunks(k), to_chunks(v), to_chunks(logf)))
    o_rest = o_rest.swapaxes(0, 1).reshape(B, T - N_META, H, HG_HEAD)
    return jnp.concatenate([o_meta, o_rest], axis=1), S


def hgrn_branch(hq, hf, hi, hg, lb, norm_g, run):
    B, T, _ = hq.shape
    shp = (B, T, HG_HEADS, HG_HEAD)
    lb_h = lb.reshape(HG_HEADS, HG_HEAD)
    q = jax.nn.silu(hq.astype(jnp.float32)).reshape(shp)
    f = lb_h + (1.0 - lb_h) * jax.nn.sigmoid(hf.astype(jnp.float32)).reshape(shp)
    k = 1.0 - f
    v = hi.astype(jnp.float32).reshape(shp)
    o, S = run(q, k, v, jnp.log(f))
    o = rmsnorm(o, norm_g) * jax.nn.silu(hg.astype(jnp.float32)).reshape(shp)
    return o.reshape(B, T, HG_W).astype(hq.dtype), S.astype(hq.dtype)


def conv_ffn(h, hist, w_up, conv_w, conv_b, w_down):
    T = h.shape[1]
    full = jnp.concatenate([hist.astype(h.dtype), h @ w_up], axis=1)
    c = conv_b + conv_w[0] * full[:, 0:T]
    for j in range(1, CONV_W):
        c = c + conv_w[j] * full[:, j:j + T]
    gate, val = jnp.split(c, 2, axis=-1)
    return (jax.nn.silu(gate) * val) @ w_down, full[:, -(CONV_W - 1):]


def decoder_layer(x, pos, prm, lb, attend, run_hgrn, pool_hist, conv_hist):
    B, T, _ = x.shape
    h = rmsnorm(x, prm["norm1_g"])
    cq, ckv, kr, u_pool, hq, hf, hi, hg, gates = split_in(h @ prm["w_in"])
    cos, sin = rope_cos_sin(pos)
    q = (rmsnorm(cq, prm["q_norm_g"]) @ prm["w_uq"]).reshape(B, T, MLA_HEADS, QK_DIM)
    q_nope = q[..., :QK_NOPE]
    q_rope = apply_rope(q[..., QK_NOPE:], cos[:, None], sin[:, None])
    c = rmsnorm(ckv, prm["kv_norm_g"])
    k_rope = apply_rope(kr, cos, sin)
    y_a = attend(q_nope, q_rope, c, k_rope, prm["w_uk"], prm["w_uv"]) @ prm["w_mla_out"]
    pooled, new_pool = pool_mix(pool_hist, u_pool, pos, prm["pool_w"], prm["pool_scale"])
    y_b = pooled @ prm["w_pool_out"]
    rec, new_hg = hgrn_branch(hq, hf, hi, hg, lb, prm["hg_norm_g"], run_hgrn)
    y_c = rec @ prm["w_hg_out"]
    g_a, g_b, g_c = jnp.split(jax.nn.sigmoid(gates), N_BRANCH, axis=-1)
    x = x + (g_a * y_a + g_b * y_b + g_c * y_c) @ prm["w_o"]
    ff, new_conv = conv_ffn(rmsnorm(x, prm["norm2_g"]), conv_hist, prm["w_up"],
                            prm["conv_w"], prm["conv_b"], prm["w_down"])
    return x + ff, (c, k_rope, new_pool, new_hg, new_conv)


def setup_inputs(seed: int = 0) -> dict:
    key = jax.random.key(seed)
    ks = list(jax.random.split(key, 40))

    def nrm(i, shape, scale):
        return jax.random.normal(ks[i], shape, jnp.float32) * scale

    n_pages = PAST_LEN // PAGE_SIZE
    n_used = DEC_BATCH * n_pages
    n_pool = n_used + max(1, n_used // 4)
    page_table = jax.random.permutation(ks[0], n_pool)[:n_used].reshape(DEC_BATCH, n_pages).astype(jnp.int32)
    return {
        "x_prompt": nrm(1, (BATCH, SEQ, D_MODEL), 1.0),
        "x_sample": nrm(2, (DEC_BATCH, DEC_SEQ, D_MODEL), 1.0),
        "cache_kv_latent": nrm(3, (DEPTH, n_pool, PAGE_SIZE, KV_LORA), 1.0),
        "cache_k_rope": nrm(4, (DEPTH, n_pool, PAGE_SIZE, QK_ROPE), 1.0),
        "page_table": page_table,
        "state_pool": nrm(5, (DEPTH, DEC_BATCH, POOL_HIST, POOL_W), 1.0),
        "state_hgrn": nrm(6, (DEPTH, DEC_BATCH, HG_HEADS, HG_HEAD, HG_HEAD), 0.5),
        "state_conv": nrm(7, (DEPTH, DEC_BATCH, CONV_W - 1, 2 * D_FF), 1.0),
        "meta_tokens": nrm(8, (N_META, D_MODEL), 1.0),
        "norm1_g": 1.0 + nrm(9, (DEPTH, D_MODEL), 0.02),
        "w_in": nrm(10, (DEPTH, D_MODEL, N_IN), D_MODEL ** -0.5),
        "q_norm_g": 1.0 + nrm(11, (DEPTH, Q_LORA), 0.02),
        "w_uq": nrm(12, (DEPTH, Q_LORA, MLA_HEADS * QK_DIM), Q_LORA ** -0.5),
        "kv_norm_g": 1.0 + nrm(13, (DEPTH, KV_LORA), 0.02),
        "w_uk": nrm(14, (DEPTH, KV_LORA, MLA_HEADS, QK_NOPE), KV_LORA ** -0.5),
        "w_uv": nrm(15, (DEPTH, KV_LORA, MLA_HEADS, V_DIM), KV_LORA ** -0.5),
        "w_mla_out": nrm(16, (DEPTH, MLA_W, D_MODEL), MLA_W ** -0.5),
        "pool_w": nrm(17, (DEPTH, POOL_GROUPS, POOL_GROUP, POOL_GROUP), POOL_GROUP ** -0.5),
        "pool_scale": 1.0 + nrm(18, (DEPTH, POOL_W), 0.1),
        "w_pool_out": nrm(19, (DEPTH, POOL_W, D_MODEL), POOL_W ** -0.5),
        "hg_lb_logits": nrm(20, (DEPTH, HG_W), 0.5),
        "hg_norm_g": 1.0 + nrm(21, (DEPTH, HG_HEAD), 0.02),
        "w_hg_out": nrm(22, (DEPTH, HG_W, D_MODEL), HG_W ** -0.5),
        "w_o": nrm(23, (DEPTH, D_MODEL, D_MODEL), D_MODEL ** -0.5),
        "norm2_g": 1.0 + nrm(24, (DEPTH, D_MODEL), 0.02),
        "w_up": nrm(25, (DEPTH, D_MODEL, 2 * D_FF), D_MODEL ** -0.5),
        "conv_w": nrm(26, (DEPTH, CONV_W, 2 * D_FF), CONV_W ** -0.5),
        "conv_b": nrm(27, (DEPTH, 2 * D_FF), 0.02),
        "w_down": nrm(28, (DEPTH, D_FF, D_MODEL), D_FF ** -0.5),
        "final_norm_g": 1.0 + nrm(29, (D_MODEL,), 0.02),
    }


def reference(x_prompt, x_sample, cache_kv_latent, cache_k_rope, page_table, state_pool,
              state_hgrn, state_conv, meta_tokens, norm1_g, w_in, q_norm_g, w_uq, kv_norm_g,
              w_uk, w_uv, w_mla_out, pool_w, pool_scale, w_pool_out, hg_lb_logits, hg_norm_g,
              w_hg_out, w_o, norm2_g, w_up, conv_w, conv_b, w_down, final_norm_g):
    B = x_prompt.shape[0]
    Bd, Ts = x_sample.shape[:2]
    past_len = page_table.shape[1] * PAGE_SIZE
    xp = jnp.concatenate([jnp.broadcast_to(meta_tokens[None].astype(x_prompt.dtype),
                                           (B, N_META, D_MODEL)), x_prompt], axis=1)
    xs = x_sample
    pos_p = jnp.arange(xp.shape[1])
    pos_s = past_len + jnp.arange(Ts)
    lb_sm = jax.nn.softmax(hg_lb_logits.astype(jnp.float32), axis=0)
    lb_all = jnp.cumsum(lb_sm, axis=0) - lb_sm[0]

    p_c, p_kr, p_pool, p_hg, p_conv = [], [], [], [], []
    s_c, s_kr, s_pool, s_hg, s_conv = [], [], [], [], []
    for l in range(DEPTH):
        prm = {
            "norm1_g": norm1_g[l], "w_in": w_in[l], "q_norm_g": q_norm_g[l], "w_uq": w_uq[l],
            "kv_norm_g": kv_norm_g[l], "w_uk": w_uk[l], "w_uv": w_uv[l], "w_mla_out": w_mla_out[l],
            "pool_w": pool_w[l], "pool_scale": pool_scale[l], "w_pool_out": w_pool_out[l],
            "hg_norm_g": hg_norm_g[l], "w_hg_out": w_hg_out[l], "w_o": w_o[l],
            "norm2_g": norm2_g[l], "w_up": w_up[l], "conv_w": conv_w[l], "conv_b": conv_b[l],
            "w_down": w_down[l],
        }
        xp, (c, kr, npool, nhg, nconv) = decoder_layer(
            xp, pos_p, prm, lb_all[l], mla_prompt, hgrn_prompt_run,
            jnp.zeros((B, POOL_HIST, POOL_W), xp.dtype),
            jnp.zeros((B, CONV_W - 1, 2 * D_FF), xp.dtype))
        p_c.append(c); p_kr.append(kr); p_pool.append(npool); p_hg.append(nhg); p_conv.append(nconv)
        attend_s = functools.partial(mla_sample, cache_c=cache_kv_latent, cache_kr=cache_k_rope,
                                     page_table=page_table, layer=l)
        run_s = functools.partial(gla_chunk, state_hgrn[l])
        xs, (c, kr, npool, nhg, nconv) = decoder_layer(
            xs, pos_s, prm, lb_all[l], attend_s, run_s, state_pool[l], state_conv[l])
        s_c.append(c); s_kr.append(kr); s_pool.append(npool); s_hg.append(nhg); s_conv.append(nconv)

    y_prompt = rmsnorm(xp, final_norm_g)[:, N_META:]
    y_sample = rmsnorm(xs, final_norm_g)
    return (y_prompt, y_sample,
            jnp.stack(p_c), jnp.stack(p_kr), jnp.stack(p_pool), jnp.stack(p_hg), jnp.stack(p_conv),
            jnp.stack(s_c), jnp.stack(s_kr), jnp.stack(s_pool), jnp.stack(s_hg), jnp.stack(s_conv))
```

```python
import functools

import numpy as np
import jax
import jax.numpy as jnp
from jax import lax
from jax.experimental import pallas as pl
from jax.experimental.pallas import tpu as pltpu

f32 = jnp.float32
bf16 = jnp.bfloat16

D_MODEL = 2048
N_META = 16
EPS = 1e-6
MLA_HEADS = 8
QK_NOPE = 128
QK_ROPE = 64
QK_DIM = QK_NOPE + QK_ROPE
V_DIM = 128
Q_LORA = 384
KV_LORA = 128
ROPE_THETA = 10000.0
SM_SCALE = QK_DIM ** -0.5
PAGE_SIZE = 128
POOL_WINDOWS = (2, 4, 8, 16)
POOL_GROUP = 256
POOL_W = 1024
POOL_HIST = 15
HG_W = 1024
HG_HEAD = 128
HG_HEADS = 8
D_FF = 5632
CONV_W = 3
N_BRANCH = 3
IN_SIZES = (Q_LORA, KV_LORA, QK_ROPE, POOL_W, HG_W, HG_W, HG_W, HG_W, N_BRANCH * D_MODEL)

Z_BLK = 1024
Z_GATES = 0
Z_HQ, Z_HF, Z_HI, Z_HG = 6, 7, 8, 9
Z_POOL = 10
Z_MLA = 11
Z_COLS = 12 * Z_BLK
QCAT = 256
KCAT = 256

V7X_VMEM_LIMIT_BYTES = 56 * 1024 * 1024
SUBLANES = 8
HG_SUB = 16
NEG_BIG = -1e30


def _params(*sem):
    return pltpu.CompilerParams(dimension_semantics=sem, vmem_limit_bytes=V7X_VMEM_LIMIT_BYTES)


def _dot(a, b):
    return jnp.dot(a, b, preferred_element_type=f32)


def _dot_nt(a, b):
    return lax.dot_general(a, b, (((1,), (1,)), ((), ())), preferred_element_type=f32)


def _dot_tn(a, b):
    return lax.dot_general(a, b, (((0,), (0,)), ((), ())), preferred_element_type=f32)


def _rms(x, g):
    return x * lax.rsqrt(jnp.mean(x * x, axis=-1, keepdims=True) + EPS) * g


def _sigmoid(x):
    return 1.0 / (1.0 + jnp.exp(-x))


def _silu(x):
    return x * _sigmoid(x)


def _norm_mm_body(x_ref, g_ref, w_ref, o_ref, h_ref):
    @pl.when(pl.program_id(1) == 0)
    def _():
        h_ref[...] = _rms(x_ref[...], g_ref[...]).astype(bf16)

    o_ref[...] = _dot(h_ref[...], w_ref[...])


def norm_matmul(x, g, w, *, tm, tn, name):
    m, k = x.shape
    n = w.shape[1]
    return pl.pallas_call(
        _norm_mm_body,
        grid=(m // tm, n // tn),
        in_specs=[
            pl.BlockSpec((tm, k), lambda i, j: (i, 0)),
            pl.BlockSpec((1, k), lambda i, j: (0, 0)),
            pl.BlockSpec((k, tn), lambda i, j: (0, j)),
        ],
        out_specs=pl.BlockSpec((tm, tn), lambda i, j: (i, j)),
        out_shape=jax.ShapeDtypeStruct((m, n), f32),
        scratch_shapes=[pltpu.VMEM((tm, k), bf16)],
        compiler_params=_params("parallel", "arbitrary"),
        name=name,
    )(x, g.reshape(1, k), w)


def _mla_prep_body(zm_ref, tab_ref, qg_ref, kvg_ref, wuq_ref, wukt_ref,
                   qcat_ref, kcat_ref, c_ref, kro_ref):
    zm = zm_ref[...]
    tab = tab_ref[...]
    lane = lax.broadcasted_iota(jnp.int32, (1, 128), 1)
    keep = lane < QK_ROPE

    def rope(blk):
        p = blk * tab
        return jnp.where(keep, p + pltpu.roll(p, QK_ROPE, 1), 0.0)

    cqn = _rms(zm[:, :Q_LORA], qg_ref[...]).astype(bf16)
    q = _dot(cqn, wuq_ref[...])
    c = _rms(zm[:, Q_LORA:Q_LORA + KV_LORA], kvg_ref[...])
    kr = rope(zm[:, Q_LORA + KV_LORA:Q_LORA + KV_LORA + 128])
    c_ref[...] = c
    kro_ref[...] = kr[:, :QK_ROPE]
    kcat_ref[:, :KV_LORA] = c.astype(bf16)
    kcat_ref[:, KV_LORA:] = kr.astype(bf16)
    for h in range(MLA_HEADS):
        qn = q[:, h * QK_NOPE:(h + 1) * QK_NOPE].astype(bf16)
        ql = _dot(qn, wukt_ref[h]) * SM_SCALE
        base = MLA_HEADS * QK_NOPE + h * 128
        qr = rope(q[:, base:base + 128]) * SM_SCALE
        qcat_ref[:, h * QCAT:h * QCAT + 128] = ql.astype(qcat_ref.dtype)
        qcat_ref[:, h * QCAT + 128:(h + 1) * QCAT] = qr.astype(qcat_ref.dtype)


def mla_prep(z, tab, qg, kvg, wuq, wukt, *, tm, q_dtype, name):
    m = z.shape[0]
    return pl.pallas_call(
        _mla_prep_body,
        grid=(m // tm,),
        in_specs=[
            pl.BlockSpec((tm, Z_BLK), lambda i: (i, Z_MLA)),
            pl.BlockSpec((tm, 128), lambda i: (i, 0)),
            pl.BlockSpec((1, Q_LORA), lambda i: (0, 0)),
            pl.BlockSpec((1, KV_LORA), lambda i: (0, 0)),
            pl.BlockSpec(wuq.shape, lambda i: (0, 0)),
            pl.BlockSpec(wukt.shape, lambda i: (0, 0, 0)),
        ],
        out_specs=[
            pl.BlockSpec((tm, MLA_HEADS * QCAT), lambda i: (i, 0)),
            pl.BlockSpec((tm, KCAT), lambda i: (i, 0)),
            pl.BlockSpec((tm, KV_LORA), lambda i: (i, 0)),
            pl.BlockSpec((tm, QK_ROPE), lambda i: (i, 0)),
        ],
        out_shape=[
            jax.ShapeDtypeStruct((m, MLA_HEADS * QCAT), q_dtype),
            jax.ShapeDtypeStruct((m, KCAT), bf16),
            jax.ShapeDtypeStruct((m, KV_LORA), f32),
            jax.ShapeDtypeStruct((m, QK_ROPE), f32),
        ],
        compiler_params=_params("parallel"),
        name=name,
    )(z, tab, qg.reshape(1, -1), kvg.reshape(1, -1), wuq, wukt)


def _attn_prompt_body(qi_ref, kj_ref, q_ref, k_ref, wuv_ref, o_ref, m_ref, l_ref, acc_ref, *, t):
    step = pl.program_id(0)
    i = qi_ref[step]
    j = kj_ref[step]

    @pl.when(j == 0)
    def _():
        m_ref[...] = jnp.full(m_ref.shape, NEG_BIG, f32)
        l_ref[...] = jnp.zeros(l_ref.shape, f32)
        acc_ref[...] = jnp.zeros(acc_ref.shape, f32)

    k = k_ref[...]
    v = k[:, :KV_LORA]
    qpos = i * t + lax.broadcasted_iota(jnp.int32, (t, 1), 0)
    kpos = j * t + lax.broadcasted_iota(jnp.int32, (1, t), 1)
    mask = kpos <= qpos
    for h in range(MLA_HEADS):
        s = _dot_nt(q_ref[:, h * QCAT:(h + 1) * QCAT], k)
        s = jnp.where(mask, s, NEG_BIG)
        m_prev = m_ref[h]
        m_new = jnp.maximum(m_prev, jnp.max(s, axis=-1, keepdims=True))
        alpha = jnp.exp(m_prev - m_new)
        p = jnp.exp(s - m_new)
        l_ref[h] = alpha * l_ref[h] + jnp.sum(p, axis=-1, keepdims=True)
        acc_ref[h] = alpha * acc_ref[h] + _dot(p.astype(bf16), v)
        m_ref[h] = m_new

    @pl.when(j == i)
    def _():
        for h in range(MLA_HEADS):
            lat = (acc_ref[h] / l_ref[h]).astype(bf16)
            o_ref[:, h * V_DIM:(h + 1) * V_DIM] = _dot(lat, wuv_ref[h]).astype(o_ref.dtype)


def attn_prompt(qcat, kcat, wuv, *, t, name):
    m = qcat.shape[0]
    nb = m // t
    pairs = [(i, j) for i in range(nb) for j in range(i + 1)]
    qi = jnp.asarray(np.array([p[0] for p in pairs], np.int32))
    kj = jnp.asarray(np.array([p[1] for p in pairs], np.int32))
    grid_spec = pltpu.PrefetchScalarGridSpec(
        num_scalar_prefetch=2,
        grid=(len(pairs),),
        in_specs=[
            pl.BlockSpec((t, MLA_HEADS * QCAT), lambda s, qi, kj: (qi[s], 0)),
            pl.BlockSpec((t, KCAT), lambda s, qi, kj: (kj[s], 0)),
            pl.BlockSpec(wuv.shape, lambda s, qi, kj: (0, 0, 0)),
        ],
        out_specs=pl.BlockSpec((t, MLA_HEADS * V_DIM), lambda s, qi, kj: (qi[s], 0)),
        scratch_shapes=[
            pltpu.VMEM((MLA_HEADS, t, 1), f32),
            pltpu.VMEM((MLA_HEADS, t, 1), f32),
            pltpu.VMEM((MLA_HEADS, t, KV_LORA), f32),
        ],
    )
    return pl.pallas_call(
        functools.partial(_attn_prompt_body, t=t),
        grid_spec=grid_spec,
        out_shape=jax.ShapeDtypeStruct((m, MLA_HEADS * V_DIM), bf16),
        compiler_params=_params("arbitrary"),
        name=name,
    )(qi, kj, qcat, kcat, wuv)


def _attn_sample_body(pt_ref, q_ref, cn_ref, krn_ref, wuv_ref, *rest, n_pg, t_new):
    c_refs = rest[:n_pg]
    kr_refs = rest[n_pg:2 * n_pg]
    o_ref = rest[2 * n_pg]
    qs_ref, m_ref, l_ref, acc_ref = rest[2 * n_pg + 1:]
    j = pl.program_id(1)
    rows = MLA_HEADS * t_new

    @pl.when(j == 0)
    def _():
        for h in range(MLA_HEADS):
            qs_ref[h * t_new:(h + 1) * t_new, :] = q_ref[:, h * QCAT:(h + 1) * QCAT]
        m_ref[...] = jnp.full(m_ref.shape, NEG_BIG, f32)
        l_ref[...] = jnp.zeros(l_ref.shape, f32)
        acc_ref[...] = jnp.zeros(acc_ref.shape, f32)

    q = qs_ref[...].astype(bf16)
    q_lat = q[:, :KV_LORA]
    q_rope = q[:, KV_LORA:KV_LORA + 128]

    def update(s_list, v_list):
        m_prev = m_ref[...]
        m_new = m_prev
        for s in s_list:
            m_new = jnp.maximum(m_new, jnp.max(s, axis=-1, keepdims=True))
        alpha = jnp.exp(m_prev - m_new)
        l_new = alpha * l_ref[...]
        acc = alpha * acc_ref[...]
        for s, v in zip(s_list, v_list):
            p = jnp.exp(s - m_new)
            l_new = l_new + jnp.sum(p, axis=-1, keepdims=True)
            acc = acc + _dot(p.astype(bf16), v)
        m_ref[...] = m_new
        l_ref[...] = l_new
        acc_ref[...] = acc

    def rope_pad(kr):
        return jnp.concatenate([kr, jnp.zeros_like(kr)], axis=1).astype(bf16)

    s_list, v_list = [], []
    for p in range(n_pg):
        c = c_refs[p][...].astype(bf16)
        s_list.append(_dot_nt(q_lat, c) + _dot_nt(q_rope, rope_pad(kr_refs[p][...])))
        v_list.append(c)
    update(s_list, v_list)

    @pl.when(j == pl.num_programs(1) - 1)
    def _():
        pad = PAGE_SIZE - t_new
        cn = jnp.concatenate([cn_ref[...], jnp.zeros((pad, KV_LORA), f32)], axis=0).astype(bf16)
        krn = jnp.concatenate([krn_ref[...], jnp.zeros((pad, QK_ROPE), f32)], axis=0)
        s = _dot_nt(q_lat, cn) + _dot_nt(q_rope, rope_pad(krn))
        tq = lax.broadcasted_iota(jnp.int32, (rows, 1), 0) % t_new
        tk = lax.broadcasted_iota(jnp.int32, (1, PAGE_SIZE), 1)
        s = jnp.where(tk <= tq, s, NEG_BIG)
        update([s], [cn])
        lat = (acc_ref[...] / l_ref[...]).astype(bf16)
        for h in range(MLA_HEADS):
            o_ref[:, h * V_DIM:(h + 1) * V_DIM] = _dot(
                lat[h * t_new:(h + 1) * t_new], wuv_ref[h]).astype(o_ref.dtype)


def attn_sample(qcat, c_new, kr_new, cache_c, cache_kr, page_table, wuv, *, layer, t_new, n_pg, name):
    n_seq, n_pages = page_table.shape
    n_chunk = n_pages // n_pg

    def page_spec(width, p):
        return pl.BlockSpec((None, None, PAGE_SIZE, width),
                            lambda b, j, pt: (layer, pt[b, j * n_pg + p], 0, 0))

    grid_spec = pltpu.PrefetchScalarGridSpec(
        num_scalar_prefetch=1,
        grid=(n_seq, n_chunk),
        in_specs=[
            pl.BlockSpec((t_new, MLA_HEADS * QCAT), lambda b, j, pt: (b, 0)),
            pl.BlockSpec((t_new, KV_LORA), lambda b, j, pt: (b, 0)),
            pl.BlockSpec((t_new, QK_ROPE), lambda b, j, pt: (b, 0)),
            pl.BlockSpec(wuv.shape, lambda b, j, pt: (0, 0, 0)),
        ] + [page_spec(KV_LORA, p) for p in range(n_pg)] + [page_spec(QK_ROPE, p) for p in range(n_pg)],
        out_specs=pl.BlockSpec((t_new, MLA_HEADS * V_DIM), lambda b, j, pt: (b, 0)),
        scratch_shapes=[
            pltpu.VMEM((MLA_HEADS * t_new, QCAT), f32),
            pltpu.VMEM((MLA_HEADS * t_new, 1), f32),
            pltpu.VMEM((MLA_HEADS * t_new, 1), f32),
            pltpu.VMEM((MLA_HEADS * t_new, KV_LORA), f32),
        ],
    )
    return pl.pallas_call(
        functools.partial(_attn_sample_body, n_pg=n_pg, t_new=t_new),
        grid_spec=grid_spec,
        out_shape=jax.ShapeDtypeStruct((n_seq * t_new, MLA_HEADS * V_DIM), f32),
        compiler_params=_params("parallel", "arbitrary"),
        name=name,
    )(page_table, qcat, c_new, kr_new, wuv, *([cache_c] * n_pg), *([cache_kr] * n_pg))


def _pool_finish(win_sum_of, u, pos, pw_ref, ps_ref, o_ref):
    for g, w in enumerate(POOL_WINDOWS):
        sl = slice(g * POOL_GROUP, (g + 1) * POOL_GROUP)
        cnt = jnp.minimum(w, pos + 1).astype(f32)
        d = (win_sum_of(g, w) / cnt - u[:, sl]).astype(bf16)
        y = _dot(d, pw_ref[g]) * ps_ref[:, sl]
        o_ref[:, sl] = y.astype(o_ref.dtype)


def _pool_prompt_body(u_ref, halo_ref, pw_ref, ps_ref, o_ref, buf_ref, *, tm, halo):
    i = pl.program_id(0)
    u = u_ref[...]
    buf_ref[:halo, :] = jnp.where(i == 0, 0.0, halo_ref[...])
    buf_ref[halo:, :] = u
    pos = i * tm + lax.broadcasted_iota(jnp.int32, (tm, 1), 0)

    def win_sum_of(g, w):
        acc = u[:, g * POOL_GROUP:(g + 1) * POOL_GROUP]
        for d in range(1, w):
            acc = acc + buf_ref[pl.ds(halo - d, tm), g * POOL_GROUP:(g + 1) * POOL_GROUP]
        return acc

    _pool_finish(win_sum_of, u, pos, pw_ref, ps_ref, o_ref)


def pool_prompt(z, pw, ps, *, tm, name):
    m = z.shape[0]
    halo = 16
    return pl.pallas_call(
        functools.partial(_pool_prompt_body, tm=tm, halo=halo),
        grid=(m // tm,),
        in_specs=[
            pl.BlockSpec((tm, Z_BLK), lambda i: (i, Z_POOL)),
            pl.BlockSpec((halo, Z_BLK), lambda i: (jnp.maximum(i * (tm // halo) - 1, 0), Z_POOL)),
            pl.BlockSpec(pw.shape, lambda i: (0, 0, 0)),
            pl.BlockSpec((1, POOL_W), lambda i: (0, 0)),
        ],
        out_specs=pl.BlockSpec((tm, POOL_W), lambda i: (i, 0)),
        out_shape=jax.ShapeDtypeStruct((m, POOL_W), bf16),
        scratch_shapes=[pltpu.VMEM((tm + halo, POOL_W), f32)],
        compiler_params=_params("arbitrary"),
        name=name,
    )(z, z, pw, ps.reshape(1, -1))


def _pool_sample_body(full_ref, pw_ref, ps_ref, o_ref, *, nb, t_new, past_len):
    lead = full_ref.shape[1] - t_new
    rows = nb * t_new
    u = full_ref[:, lead:, :].reshape(rows, POOL_W)
    pos = past_len + lax.broadcasted_iota(jnp.int32, (rows, 1), 0) % t_new

    def win_sum_of(g, w):
        sl = slice(g * POOL_GROUP, (g + 1) * POOL_GROUP)
        acc = full_ref[:, lead:, sl]
        for d in range(1, w):
            acc = acc + full_ref[:, lead - d:lead - d + t_new, sl]
        return acc.reshape(rows, POOL_GROUP)

    _pool_finish(win_sum_of, u, pos, pw_ref, ps_ref, o_ref)


def pool_sample(full, pw, ps, *, nb, t_new, past_len, name):
    n_seq, rows_per, _ = full.shape
    return pl.pallas_call(
        functools.partial(_pool_sample_body, nb=nb, t_new=t_new, past_len=past_len),
        grid=(n_seq // nb,),
        in_specs=[
            pl.BlockSpec((nb, rows_per, POOL_W), lambda i: (i, 0, 0)),
            pl.BlockSpec(pw.shape, lambda i: (0, 0, 0)),
            pl.BlockSpec((1, POOL_W), lambda i: (0, 0)),
        ],
        out_specs=pl.BlockSpec((nb * t_new, POOL_W), lambda i: (i, 0)),
        out_shape=jax.ShapeDtypeStruct((n_seq * t_new, POOL_W), bf16),
        compiler_params=_params("parallel"),
        name=name,
    )(full, pw, ps.reshape(1, -1))


def _cumsum_rows8(x):
    r = lax.broadcasted_iota(jnp.int32, (SUBLANES, 1), 0)
    for s in (1, 2, 4):
        x = x + jnp.where(r >= s, pltpu.roll(x, s, 0), 0.0)
    return x


def _cumsum_rows(x):
    parts, carry = [], None
    for a in range(0, x.shape[0], SUBLANES):
        c = _cumsum_rows8(x[a:a + SUBLANES])
        if carry is not None:
            c = c + carry
        carry = c[SUBLANES - 1:SUBLANES]
        parts.append(c)
    return parts[0] if len(parts) == 1 else jnp.concatenate(parts, axis=0)


def _lower_bound(logit_ref, layer):
    lg = logit_ref[...]
    e = jnp.exp(lg - jnp.max(lg, axis=0, keepdims=True))
    sm = e / jnp.sum(e, axis=0, keepdims=True)
    cs = sm[0:1]
    for l in range(1, layer + 1):
        cs = cs + sm[l:l + 1]
    return cs - sm[0:1]


def _hgrn_block(hq, hf, hi, hg, lb, ng, st_refs, o_ref, row0):
    c_rows = hq.shape[0]
    q = _silu(hq)
    f = lb + (1.0 - lb) * _sigmoid(hf)
    k = 1.0 - f
    b = _cumsum_rows(jnp.log(f))
    gate = _silu(hg)
    ri = lax.broadcasted_iota(jnp.int32, (c_rows, 1), 0)
    for h in range(HG_HEADS):
        sl = slice(h * HG_HEAD, (h + 1) * HG_HEAD)
        qh, kh, vh, bh = q[:, sl], k[:, sl], hi[:, sl], b[:, sl]
        st = st_refs[h]
        s_t = st[...]
        o = _dot_nt((qh * jnp.exp(bh)).astype(bf16), s_t.astype(bf16))
        for j in range(c_rows):
            e = jnp.exp(jnp.where(ri >= j, bh - bh[j:j + 1], -jnp.inf))
            a = jnp.sum(qh * kh[j:j + 1] * e, axis=-1, keepdims=True)
            o = o + a * vh[j:j + 1]
        bl = bh[c_rows - 1:c_rows]
        kt = (kh * jnp.exp(bl - bh)).astype(bf16)
        st[...] = s_t * jnp.exp(bl) + _dot_tn(vh.astype(bf16), kt)
        y = _rms(o, ng) * gate[:, sl]
        o_ref[pl.ds(row0, c_rows), sl] = y.astype(o_ref.dtype)


def _hgrn_prompt_body(hq_ref, hf_ref, hi_ref, hg_ref, lg_ref, ng_ref, o_ref, s_out_ref, st_ref,
                      *, layer, tm, n_valid):
    i = pl.program_id(0)

    @pl.when(i == 0)
    def _():
        st_ref[...] = jnp.zeros(st_ref.shape, f32)

    lb = _lower_bound(lg_ref, layer)
    ng = ng_ref[...]
    st_refs = [st_ref.at[h] for h in range(HG_HEADS)]
    n_sub = tm // HG_SUB
    last_sub = n_valid // HG_SUB - 1

    def sub(s, carry):
        r0 = pl.multiple_of(s * HG_SUB, HG_SUB)
        rows = pl.ds(r0, HG_SUB)
        _hgrn_block(hq_ref[rows, :], hf_ref[rows, :], hi_ref[rows, :], hg_ref[rows, :],
                    lb, ng, st_refs, o_ref, r0)

        @pl.when(i * n_sub + s == last_sub)
        def _():
            for h in range(HG_HEADS):
                s_out_ref[h] = st_ref[h].T
        return carry

    lax.fori_loop(0, n_sub, sub, 0)


def hgrn_prompt(z, logits, ng, *, layer, tm, n_valid, name):
    m = z.shape[0]
    zspec = lambda blk: pl.BlockSpec((tm, Z_BLK), lambda i: (i, blk))
    return pl.pallas_call(
        functools.partial(_hgrn_prompt_body, layer=layer, tm=tm, n_valid=n_valid),
        grid=(m // tm,),
        in_specs=[zspec(Z_HQ), zspec(Z_HF), zspec(Z_HI), zspec(Z_HG),
                  pl.BlockSpec(logits.shape, lambda i: (0, 0)),
                  pl.BlockSpec((1, HG_HEAD), lambda i: (0, 0))],
        out_specs=[pl.BlockSpec((tm, HG_W), lambda i: (i, 0)),
                   pl.BlockSpec((HG_HEADS, HG_HEAD, HG_HEAD), lambda i: (0, 0, 0))],
        out_shape=[jax.ShapeDtypeStruct((m, HG_W), bf16),
                   jax.ShapeDtypeStruct((HG_HEADS, HG_HEAD, HG_HEAD), f32)],
        scratch_shapes=[pltpu.VMEM((HG_HEADS, HG_HEAD, HG_HEAD), f32)],
        compiler_params=_params("arbitrary"),
        name=name,
    )(z, z, z, z, logits, ng.reshape(1, -1))


def _hgrn_sample_body(hq_ref, hf_ref, hi_ref, hg_ref, lg_ref, ng_ref, s_in_ref, o_ref, s_out_ref, st_ref,
                      *, layer, nb, t_new):
    lb = _lower_bound(lg_ref, layer)
    ng = ng_ref[...]
    st_refs = [st_ref.at[h] for h in range(HG_HEADS)]
    for n in range(nb):
        for h in range(HG_HEADS):
            st_ref[h] = s_in_ref[n, h].T
        rows = slice(n * t_new, (n + 1) * t_new)
        _hgrn_block(hq_ref[rows, :], hf_ref[rows, :], hi_ref[rows, :], hg_ref[rows, :],
                    lb, ng, st_refs, o_ref, n * t_new)
        for h in range(HG_HEADS):
            s_out_ref[n, h] = st_ref[h].T


def hgrn_sample(z, logits, ng, state, *, layer, nb, t_new, name):
    m = z.shape[0]
    n_seq = m // t_new
    rows = nb * t_new
    zspec = lambda blk: pl.BlockSpec((rows, Z_BLK), lambda i: (i, blk))
    sspec = pl.BlockSpec((nb, HG_HEADS, HG_HEAD, HG_HEAD), lambda i: (i, 0, 0, 0))
    return pl.pallas_call(
        functools.partial(_hgrn_sample_body, layer=layer, nb=nb, t_new=t_new),
        grid=(n_seq // nb,),
        in_specs=[zspec(Z_HQ), zspec(Z_HF), zspec(Z_HI), zspec(Z_HG),
                  pl.BlockSpec(logits.shape, lambda i: (0, 0)),
                  pl.BlockSpec((1, HG_HEAD), lambda i: (0, 0)),
                  sspec],
        out_specs=[pl.BlockSpec((rows, HG_W), lambda i: (i, 0)), sspec],
        out_shape=[jax.ShapeDtypeStruct((m, HG_W), f32),
                   jax.ShapeDtypeStruct(state.shape, f32)],
        scratch_shapes=[pltpu.VMEM((HG_HEADS, HG_HEAD, HG_HEAD), f32)],
        compiler_params=_params("parallel"),
        name=name,
    )(z, z, z, z, logits, ng.reshape(1, -1), state)


def _merge_body(ya_ref, yb_ref, yc_ref, wa_ref, wb_ref, wc_ref, ga_ref, gb_ref, gc_ref, o_ref):
    mix = _sigmoid(ga_ref[...]) * _dot(ya_ref[...].astype(bf16), wa_ref[...])
    mix = mix + _sigmoid(gb_ref[...]) * _dot(yb_ref[...], wb_ref[...])
    mix = mix + _sigmoid(gc_ref[...]) * _dot(yc_ref[...].astype(bf16), wc_ref[...])
    o_ref[...] = mix.astype(o_ref.dtype)


def branch_merge(ya, yb, yc, wa, wb, wc, z, *, tm, tn, name):
    m, k = yb.shape
    n = wa.shape[1]
    nj = n // tn
    yspec = pl.BlockSpec((tm, k), lambda i, j: (i, 0))
    wspec = pl.BlockSpec((k, tn), lambda i, j: (0, j))
    gspec = lambda br: pl.BlockSpec((tm, tn), lambda i, j: (i, br * nj + j))
    return pl.pallas_call(
        _merge_body,
        grid=(m // tm, nj),
        in_specs=[yspec, yspec, yspec, wspec, wspec, wspec, gspec(0), gspec(1), gspec(2)],
        out_specs=pl.BlockSpec((tm, tn), lambda i, j: (i, j)),
        out_shape=jax.ShapeDtypeStruct((m, n), bf16),
        compiler_params=_params("parallel", "arbitrary"),
        name=name,
    )(ya, yb, yc, wa, wb, wc, z, z, z)


def _mm_res_body(a_ref, w_ref, r_ref, o_ref):
    o_ref[...] = r_ref[...] + _dot(a_ref[...], w_ref[...])


def matmul_residual(a, w, res, *, tm, tn, name):
    m, k = a.shape
    n = w.shape[1]
    return pl.pallas_call(
        _mm_res_body,
        grid=(m // tm, n // tn),
        in_specs=[pl.BlockSpec((tm, k), lambda i, j: (i, 0)),
                  pl.BlockSpec((k, tn), lambda i, j: (0, j)),
                  pl.BlockSpec((tm, tn), lambda i, j: (i, j))],
        out_specs=pl.BlockSpec((tm, tn), lambda i, j: (i, j)),
        out_shape=jax.ShapeDtypeStruct((m, n), f32),
        compiler_params=_params("parallel", "arbitrary"),
        name=name,
    )(a, w, res)


def _conv_taps(up, p1, p2, cw_ref, cb_ref):
    cw = cw_ref[...]
    return cb_ref[...] + cw[0:1] * p2 + cw[1:2] * p1 + cw[2:3] * up


def _ffn_down_finish(act, wd_ref, x_ref, o_ref):
    part = _dot(act.astype(bf16), wd_ref[...])

    @pl.when(pl.program_id(1) == 0)
    def _():
        o_ref[...] = x_ref[...] + part

    @pl.when(pl.program_id(1) > 0)
    def _():
        o_ref[...] += part


def _ffn_down_prompt_body(upg_ref, upv_ref, hg_ref, hv_ref, cwg_ref, cwv_ref, cbg_ref, cbv_ref,
                          wd_ref, x_ref, o_ref, *, tm):
    first = pl.program_id(0) == 0
    r = lax.broadcasted_iota(jnp.int32, (tm, 1), 0)

    def conv(up_ref, halo_ref, cw_ref, cb_ref):
        up = up_ref[...]
        halo = halo_ref[...]
        h1 = jnp.where(first, 0.0, halo[SUBLANES - 1:SUBLANES])
        h2 = jnp.where(first, 0.0, halo[SUBLANES - 2:SUBLANES - 1])
        p1 = jnp.where(r == 0, h1, pltpu.roll(up, 1, 0))
        p2 = jnp.where(r == 0, h2, jnp.where(r == 1, h1, pltpu.roll(up, 2, 0)))
        return _conv_taps(up, p1, p2, cw_ref, cb_ref)

    gate = conv(upg_ref, hg_ref, cwg_ref, cbg_ref)
    val = conv(upv_ref, hv_ref, cwv_ref, cbv_ref)
    _ffn_down_finish(_silu(gate) * val, wd_ref, x_ref, o_ref)


def _ffn_down_sample_body(upg_ref, upv_ref, h0g_ref, h1g_ref, h0v_ref, h1v_ref, cwg_ref, cwv_ref,
                          cbg_ref, cbv_ref, wd_ref, x_ref, o_ref, *, tm, t_new):
    t = lax.broadcasted_iota(jnp.int32, (tm, 1), 0) % t_new
    n_seq = tm // t_new

    def per_seq(h_ref):
        h = h_ref[...]
        return jnp.broadcast_to(h[:, None, :], (n_seq, t_new, h.shape[1])).reshape(tm, h.shape[1])

    def conv(up_ref, h0_ref, h1_ref, cw_ref, cb_ref):
        up = up_ref[...]
        h0 = per_seq(h0_ref)
        h1 = per_seq(h1_ref)
        p1 = jnp.where(t == 0, h1, pltpu.roll(up, 1, 0))
        p2 = jnp.where(t == 0, h0, jnp.where(t == 1, h1, pltpu.roll(up, 2, 0)))
        return _conv_taps(up, p1, p2, cw_ref, cb_ref)

    gate = conv(upg_ref, h0g_ref, h1g_ref, cwg_ref, cbg_ref)
    val = conv(upv_ref, h0v_ref, h1v_ref, cwv_ref, cbv_ref)
    _ffn_down_finish(_silu(gate) * val, wd_ref, x_ref, o_ref)


def ffn_down(up, conv_w, conv_b, wd, x, *, tm, tk, hist=None, t_new=None, name):
    m = up.shape[0]
    n = wd.shape[1]
    nk = D_FF // tk
    gcol = lambda i, k: (i, k)
    vcol = lambda i, k: (i, nk + k)
    up_specs = [pl.BlockSpec((tm, tk), gcol), pl.BlockSpec((tm, tk), vcol)]
    tail_specs = [
        pl.BlockSpec((CONV_W, tk), lambda i, k: (0, k)),
        pl.BlockSpec((CONV_W, tk), lambda i, k: (0, nk + k)),
        pl.BlockSpec((1, tk), lambda i, k: (0, k)),
        pl.BlockSpec((1, tk), lambda i, k: (0, nk + k)),
        pl.BlockSpec((tk, n), lambda i, k: (k, 0)),
        pl.BlockSpec((tm, n), lambda i, k: (i, 0)),
    ]
    cb = conv_b.reshape(1, -1)
    if hist is None:
        prev = lambda i: jnp.maximum(i * (tm // SUBLANES) - 1, 0)
        mid_specs = [pl.BlockSpec((SUBLANES, tk), lambda i, k: (prev(i), k)),
                     pl.BlockSpec((SUBLANES, tk), lambda i, k: (prev(i), nk + k))]
        body = functools.partial(_ffn_down_prompt_body, tm=tm)
        args = (up, up, up, up, conv_w, conv_w, cb, cb, wd, x)
    else:
        h0, h1 = hist
        n_seq = h0.shape[0]
        assert tm == m == n_seq * t_new
        hg = pl.BlockSpec((n_seq, tk), lambda i, k: (0, k))
        hv = pl.BlockSpec((n_seq, tk), lambda i, k: (0, nk + k))
        mid_specs = [hg, hg, hv, hv]
        body = functools.partial(_ffn_down_sample_body, tm=tm, t_new=t_new)
        args = (up, up, h0, h1, h0, h1, conv_w, conv_w, cb, cb, wd, x)
    return pl.pallas_call(
        body,
        grid=(m // tm, nk),
        in_specs=up_specs + mid_specs + tail_specs,
        out_specs=pl.BlockSpec((tm, n), lambda i, k: (i, 0)),
        out_shape=jax.ShapeDtypeStruct((m, n), f32),
        compiler_params=_params("parallel", "arbitrary"),
        name=name,
    )(*args)


def _rmsnorm_body(x_ref, g_ref, o_ref):
    o_ref[...] = _rms(x_ref[...], g_ref[...])


def rmsnorm_rows(x, g, *, tm, name):
    m, k = x.shape
    return pl.pallas_call(
        _rmsnorm_body,
        grid=(m // tm,),
        in_specs=[pl.BlockSpec((tm, k), lambda i: (i, 0)), pl.BlockSpec((1, k), lambda i: (0, 0))],
        out_specs=pl.BlockSpec((tm, k), lambda i: (i, 0)),
        out_shape=jax.ShapeDtypeStruct((m, k), f32),
        compiler_params=_params("parallel"),
        name=name,
    )(x, g.reshape(1, k))


def _swap_halves(a):
    half = a.shape[-1] // 2
    return jnp.concatenate([a[..., half:], a[..., :half]], axis=-1)


def _prep_w_in(w_in):
    idx = np.cumsum(IN_SIZES)[:-1].tolist()
    cq, ckv, kr, u, hq, hf, hi, hg, gates = jnp.split(w_in, idx, axis=-1)
    pad = jnp.zeros((w_in.shape[0], Z_COLS - sum(IN_SIZES) - QK_ROPE), w_in.dtype)
    return jnp.concatenate([gates, hq, hf, hi, hg, u, cq, ckv, kr, _swap_halves(kr), pad],
                           axis=-1).astype(bf16)


def _prep_w_uq(w_uq):
    w = w_uq.reshape(Q_LORA, MLA_HEADS, QK_DIM)
    nope = w[:, :, :QK_NOPE].reshape(Q_LORA, MLA_HEADS * QK_NOPE)
    r = w[:, :, QK_NOPE:]
    pairs = jnp.concatenate([r, _swap_halves(r)], axis=-1).reshape(Q_LORA, MLA_HEADS * 2 * QK_ROPE)
    return jnp.concatenate([nope, pairs], axis=-1).astype(bf16)


def _rope_table(pos):
    inv = ROPE_THETA ** (-jnp.arange(0, QK_ROPE, 2, dtype=jnp.float32) / QK_ROPE)
    ang = pos.astype(jnp.float32)[:, None] * inv[None, :]
    cos, sin = jnp.cos(ang), jnp.sin(ang)
    return jnp.concatenate([cos, cos, -sin, sin], axis=-1)


def _layer(xg, z_tm, prm, layer, group):
    x = xg
    m = x.shape[0]
    z = norm_matmul(x, prm["norm1_g"], prm["w_in"], tm=z_tm, tn=1024, name=f"in_proj_{group['tag']}{layer}")
    qcat, kcat, c, kro = mla_prep(z, group["rope_tab"], prm["q_norm_g"], prm["kv_norm_g"], prm["w_uq"],
                                  prm["w_ukt"], tm=group["prep_tm"], q_dtype=group["q_dtype"],
                                  name=f"mla_prep_{group['tag']}{layer}")
    ya, yb, yc, new_hg = group["branches"](z, qcat, kcat, c, kro, prm, layer)
    mix = branch_merge(ya, yb, yc, prm["w_mla_out"], prm["w_pool_out"], prm["w_hg_out"], z,
                       tm=z_tm, tn=512, name=f"merge_{group['tag']}{layer}")
    x = matmul_residual(mix, prm["w_o"], x, tm=z_tm, tn=1024, name=f"w_o_{group['tag']}{layer}")
    up = norm_matmul(x, prm["norm2_g"], prm["w_up"], tm=z_tm, tn=1024, name=f"ffn_up_{group['tag']}{layer}")
    x = group["ffn_down"](up, prm, x, layer)
    return x, z, c, kro, new_hg, up


def kernel(x_prompt, x_sample, cache_kv_latent, cache_k_rope, page_table, state_pool, state_hgrn, state_conv, meta_tokens, norm1_g, w_in, q_norm_g, w_uq, kv_norm_g, w_uk, w_uv, w_mla_out, pool_w, pool_scale, w_pool_out, hg_lb_logits, hg_norm_g, w_hg_out, w_o, norm2_g, w_up, conv_w, conv_b, w_down, final_norm_g):
    depth = w_in.shape[0]
    batch, seq, _ = x_prompt.shape
    assert batch == 1
    n_seq, t_new, _ = x_sample.shape
    past_len = page_table.shape[1] * PAGE_SIZE
    tp = N_META + seq
    tile_p = 1040
    attn_t = 640
    pad_to = int(np.lcm(tile_p, attn_t))
    mp = -(-tp // pad_to) * pad_to
    assert mp % attn_t == 0 and mp % tile_p == 0 and tp % HG_SUB == 0
    ms = n_seq * t_new

    xp = jnp.concatenate([meta_tokens.astype(f32), x_prompt[0], jnp.zeros((mp - tp, D_MODEL), f32)], axis=0)
    xs = x_sample.reshape(ms, D_MODEL)
    tab_p = _rope_table(jnp.arange(mp))
    tab_s = _rope_table(jnp.tile(past_len + jnp.arange(t_new), n_seq))
    logits = hg_lb_logits.astype(f32)

    def prompt_branches(z, qcat, kcat, c, kro, prm, layer):
        ya = attn_prompt(qcat, kcat, prm["w_uv"], t=attn_t, name=f"attn_p{layer}")
        yb = pool_prompt(z, prm["pool_w"], prm["pool_scale"], tm=attn_t, name=f"pool_p{layer}")
        yc, s_new = hgrn_prompt(z, logits, prm["hg_norm_g"], layer=layer, tm=128, n_valid=tp,
                                name=f"hgrn_p{layer}")
        return ya, yb, yc, s_new

    def prompt_ffn_down(up, prm, x, layer):
        return ffn_down(up, prm["conv_w"], prm["conv_b"], prm["w_down"], x, tm=520, tk=512,
                        name=f"ffn_down_p{layer}")

    def sample_branches(z, qcat, kcat, c, kro, prm, layer):
        ya = attn_sample(qcat, c, kro, cache_kv_latent, cache_k_rope, page_table, prm["w_uv"],
                         layer=layer, t_new=t_new, n_pg=16, name=f"attn_s{layer}")
        u = z[:, Z_POOL * Z_BLK:(Z_POOL + 1) * Z_BLK].reshape(n_seq, t_new, POOL_W)
        full = jnp.concatenate([jnp.zeros((n_seq, 1, POOL_W), f32), state_pool[layer], u], axis=1)
        yb = pool_sample(full, prm["pool_w"], prm["pool_scale"], nb=32, t_new=t_new, past_len=past_len,
                         name=f"pool_s{layer}")
        yc, s_new = hgrn_sample(z, logits, prm["hg_norm_g"], state_hgrn[layer], layer=layer, nb=4,
                                t_new=t_new, name=f"hgrn_s{layer}")
        return ya, yb, yc, s_new

    def sample_ffn_down(up, prm, x, layer):
        hist = (state_conv[layer][:, 0, :], state_conv[layer][:, 1, :])
        return ffn_down(up, prm["conv_w"], prm["conv_b"], prm["w_down"], x, tm=ms, tk=256, hist=hist,
                        t_new=t_new, name=f"ffn_down_s{layer}")

    gp = dict(tag="p", rope_tab=tab_p, prep_tm=tile_p, q_dtype=bf16, branches=prompt_branches,
              ffn_down=prompt_ffn_down)
    gs = dict(tag="s", rope_tab=tab_s, prep_tm=512, q_dtype=f32, branches=sample_branches,
              ffn_down=sample_ffn_down)

    outs_p = {k: [] for k in ("c", "kr", "pool", "hg", "conv")}
    outs_s = {k: [] for k in ("c", "kr", "pool", "hg", "conv")}
    for l in range(depth):
        prm = {
            "norm1_g": norm1_g[l], "w_in": _prep_w_in(w_in[l]), "q_norm_g": q_norm_g[l],
            "w_uq": _prep_w_uq(w_uq[l]), "kv_norm_g": kv_norm_g[l],
            "w_ukt": jnp.transpose(w_uk[l], (1, 2, 0)).astype(bf16),
            "w_uv": jnp.transpose(w_uv[l], (1, 0, 2)).astype(bf16),
            "w_mla_out": w_mla_out[l].astype(bf16), "pool_w": pool_w[l].astype(bf16),
            "pool_scale": pool_scale[l], "w_pool_out": w_pool_out[l].astype(bf16),
            "hg_norm_g": hg_norm_g[l], "w_hg_out": w_hg_out[l].astype(bf16), "w_o": w_o[l].astype(bf16),
            "norm2_g": norm2_g[l], "w_up": w_up[l].astype(bf16), "conv_w": conv_w[l], "conv_b": conv_b[l],
            "w_down": w_down[l].astype(bf16),
        }
        xp, z, c, kro, s_new, up = _layer(xp, tile_p, prm, l, gp)
        outs_p["c"].append(c[:tp][None])
        outs_p["kr"].append(kro[:tp][None])
        outs_p["pool"].append(z[tp - POOL_HIST:tp, Z_POOL * Z_BLK:(Z_POOL + 1) * Z_BLK][None])
        outs_p["hg"].append(s_new[None])
        outs_p["conv"].append(up[tp - (CONV_W - 1):tp][None])

        xs, z, c, kro, s_new, up = _layer(xs, ms, prm, l, gs)
        u = z[:, Z_POOL * Z_BLK:(Z_POOL + 1) * Z_BLK].reshape(n_seq, t_new, POOL_W)
        outs_s["c"].append(c.reshape(n_seq, t_new, KV_LORA))
        outs_s["kr"].append(kro.reshape(n_seq, t_new, QK_ROPE))
        outs_s["pool"].append(jnp.concatenate([state_pool[l], u], axis=1)[:, -POOL_HIST:])
        outs_s["hg"].append(s_new)
        outs_s["conv"].append(up.reshape(n_seq, t_new, 2 * D_FF)[:, -(CONV_W - 1):])

    y_prompt = rmsnorm_rows(xp, final_norm_g, tm=tile_p, name="final_norm_p")[N_META:tp][None]
    y_sample = rmsnorm_rows(xs, final_norm_g, tm=ms, name="final_norm_s").reshape(n_seq, t_new, D_MODEL)
    stack = lambda d, k: jnp.stack(d[k])
    return (y_prompt, y_sample,
            stack(outs_p, "c"), stack(outs_p, "kr"), stack(outs_p, "pool"), stack(outs_p, "hg"),
            stack(outs_p, "conv"),
            stack(outs_s, "c"), stack(outs_s, "kr"), stack(outs_s, "pool"), stack(outs_s, "hg"),
            stack(outs_s, "conv"))
```
